```python
import math
import jax, jax.numpy as jnp
from jax import lax
import numpy as np

D_MODEL = 4096
BATCH = 2
SEQ = 8192
DEPTH = 1

D_MIX = D_MODEL
D_ATTN = D_MIX // 2
HEAD_DIM = 128
N_HEADS = D_ATTN // HEAD_DIM
D_POOL = D_MIX - D_ATTN
POOL_SIZES = (2, 4, 8, 16)
N_POOL_GROUPS = len(POOL_SIZES)
POOL_GROUP_DIM = D_POOL // N_POOL_GROUPS
DILATED_PATTERNS = ((128, 1), (512, 4), (2048, 16))
Q_BLOCK = 128
EPS = 1e-6

kernel_name = "hybrid_dilated_attn_pool_adaln_block"


def rmsnorm(x, gain):
    xf = x.astype(jnp.float32)
    r = lax.rsqrt(jnp.mean(xf * xf, axis=-1, keepdims=True) + EPS)
    return (xf * r * gain.astype(jnp.float32)).astype(x.dtype)


def alibi_slopes(n_heads):
    return 2.0 ** (-8.0 * jnp.arange(1, n_heads + 1, dtype=jnp.float32) / n_heads)


def dilated_window_attention(q, k, v, window, dilation, slopes):
    B, S, H, E = q.shape
    span = window // dilation
    assert span <= Q_BLOCK
    L = S // dilation
    nb = -(-L // Q_BLOCK)
    Lp = nb * Q_BLOCK

    def to_residue(t):
        return t.reshape(B, L, dilation, H, E).transpose(0, 2, 3, 1, 4)

    z3 = ((0, 0), (0, 0), (0, 0))
    qr = jnp.pad(to_residue(q), z3 + ((0, Lp - L), (0, 0)))
    kr = jnp.pad(to_residue(k), z3 + ((Q_BLOCK, Lp - L), (0, 0)))
    vr = jnp.pad(to_residue(v), z3 + ((Q_BLOCK, Lp - L), (0, 0)))
    qb = qr.reshape(B, dilation, H, nb, Q_BLOCK, E)
    kb = kr.reshape(B, dilation, H, nb + 1, Q_BLOCK, E)
    vb = vr.reshape(B, dilation, H, nb + 1, Q_BLOCK, E)
    kb = jnp.concatenate([kb[:, :, :, :-1], kb[:, :, :, 1:]], axis=-2)
    vb = jnp.concatenate([vb[:, :, :, :-1], vb[:, :, :, 1:]], axis=-2)

    qi = jnp.arange(Q_BLOCK)[:, None]
    ki = jnp.arange(2 * Q_BLOCK)[None, :]
    rel = Q_BLOCK + qi - ki
    key_idx = jnp.arange(nb)[:, None, None] * Q_BLOCK + ki[None] - Q_BLOCK
    valid = (rel >= 0)[None] & (rel <= span)[None] & (key_idx >= 0)
    bias = -slopes[:, None, None] * (rel * dilation).astype(jnp.float32)[None]

    scale = 1.0 / math.sqrt(E)
    s = jnp.einsum('brhnqe,brhnke->brhnqk', qb, kb).astype(jnp.float32) * scale
    s = s + bias[:, None]
    s = jnp.where(valid, s, -jnp.inf)
    lse = jax.nn.logsumexp(s, axis=-1)
    p = jnp.exp(s - lse[..., None])
    o = jnp.einsum('brhnqk,brhnke->brhnqe', p.astype(v.dtype), vb)
    o = o.reshape(B, dilation, H, Lp, E)[:, :, :, :L]
    lse = lse.reshape(B, dilation, H, Lp)[:, :, :, :L]
    o = o.transpose(0, 3, 1, 2, 4).reshape(B, S, H, E)
    lse = lse.transpose(0, 3, 1, 2).reshape(B, S, H)
    return o, lse


def longnet_mixture(q, k, v):
    slopes = alibi_slopes(q.shape[2])
    outs, lses = [], []
    for window, dilation in DILATED_PATTERNS:
        o, lse = dilated_window_attention(q, k, v, window, dilation, slopes)
        outs.append(o)
        lses.append(lse)
    w = jax.nn.softmax(jnp.stack(lses, axis=0), axis=0)
    o = jnp.stack(outs, axis=0).astype(jnp.float32)
    return jnp.sum(w[..., None] * o, axis=0).astype(q.dtype)


def multiscale_pool(u, w_pool, pool_scale):
    B, S, _ = u.shape
    ug = u.reshape(B, S, N_POOL_GROUPS, POOL_GROUP_DIM)
    outs = []
    for g, w in enumerate(POOL_SIZES):
        xg = ug[:, :, g].astype(jnp.float32)
        cs = jnp.pad(jnp.cumsum(xg, axis=1), ((0, 0), (1, 0), (0, 0)))
        lagged = jnp.pad(cs, ((0, 0), (w - 1, 0), (0, 0)))[:, :S]
        count = jnp.minimum(jnp.arange(1, S + 1), w).astype(jnp.float32)[None, :, None]
        pooled = (cs[:, 1:] - lagged) / count - xg
        outs.append(jnp.einsum('bsc,cd->bsd', pooled.astype(u.dtype), w_pool[g]))
    return jnp.concatenate(outs, axis=-1) * pool_scale


def setup_inputs(seed: int = 0) -> dict:
    key = jax.random.key(seed)
    ks = jax.random.split(key, 12)
    f32 = jnp.float32
    x = jax.random.normal(ks[0], (BATCH, SEQ, D_MODEL), f32)
    c = jax.random.normal(ks[1], (BATCH, D_MODEL), f32)
    norm_gain = 1.0 + 0.05 * jax.random.normal(ks[2], (DEPTH, D_MODEL), f32)
    w_ada = jax.random.normal(ks[3], (DEPTH, D_MODEL, 3 * D_MODEL), f32) * D_MODEL ** -0.5
    b_ada = 0.02 * jax.random.normal(ks[4], (DEPTH, 3 * D_MODEL), f32)
    n_in = 4 * D_ATTN + 2 * D_POOL
    w_in = jax.random.normal(ks[5], (DEPTH, D_MODEL, n_in), f32) * D_MODEL ** -0.5
    w_pool = jax.random.normal(ks[6], (DEPTH, N_POOL_GROUPS, POOL_GROUP_DIM, POOL_GROUP_DIM), f32) * POOL_GROUP_DIM ** -0.5
    pool_scale = 0.5 + 0.1 * jax.random.normal(ks[7], (DEPTH, D_POOL), f32)
    w_out = jax.random.normal(ks[8], (DEPTH, D_MIX, D_MODEL), f32) * D_MIX ** -0.5
    final_gain = 1.0 + 0.05 * jax.random.normal(ks[9], (D_MODEL,), f32)
    return {"x": x, "c": c, "norm_gain": norm_gain, "w_ada": w_ada, "b_ada": b_ada,
            "w_in": w_in, "w_pool": w_pool, "pool_scale": pool_scale,
            "w_out": w_out, "final_gain": final_gain}


def reference(x, c, norm_gain, w_ada, b_ada, w_in, w_pool, pool_scale, w_out, final_gain):
    B, S, _ = x.shape
    c_act = jax.nn.silu(c)
    for l in range(DEPTH):
        mod = c_act @ w_ada[l] + b_ada[l]
        shift, scale, gate = jnp.split(mod, 3, axis=-1)
        h = rmsnorm(x, norm_gain[l])
        h = h * (1.0 + scale[:, None, :]) + shift[:, None, :]
        proj = jnp.einsum('bsd,dn->bsn', h, w_in[l])
        q, k, v, g_attn, u_pool, g_pool = jnp.split(
            proj, np.cumsum([D_ATTN, D_ATTN, D_ATTN, D_ATTN, D_POOL]).tolist(), axis=-1)
        q = q.reshape(B, S, N_HEADS, HEAD_DIM)
        k = k.reshape(B, S, N_HEADS, HEAD_DIM)
        v = v.reshape(B, S, N_HEADS, HEAD_DIM)
        y_attn = longnet_mixture(q, k, v).reshape(B, S, D_ATTN) * jax.nn.silu(g_attn)
        y_pool = multiscale_pool(u_pool, w_pool[l], pool_scale[l]) * jax.nn.silu(g_pool)
        y = jnp.concatenate([y_attn, y_pool], axis=-1) @ w_out[l]
        x = x + gate[:, None, :] * y
    return rmsnorm(x, final_gain)
```

```python
import functools
import math

import jax
import jax.numpy as jnp
from jax import lax
from jax.experimental import pallas as pl
from jax.experimental.pallas import tpu as pltpu

F32 = jnp.float32
BF16 = jnp.bfloat16

HEAD_DIM = 128
POOL_SIZES = (2, 4, 8, 16)
DILATED_PATTERNS = ((128, 1), (512, 4), (2048, 16))
Q_BLOCK = 128
EPS = 1e-6

V7X_VMEM_LIMIT_BYTES = 58 * 1024 * 1024

ATTN_ROWS = 2048
ATTN_TAIL = Q_BLOCK


def _silu(v):
    return v * (1.0 / (1.0 + jnp.exp(-v)))


def _mod_kernel(c_ref, w_ref, b_ref, o_ref):
    ca = _silu(c_ref[...]).astype(BF16)
    w = w_ref[...].astype(BF16)
    o_ref[...] = jnp.dot(ca, w, preferred_element_type=F32) + b_ref[...]


def _adaln_mod(c, w_ada, b_ada, tn=512):
    B, D = c.shape
    N = w_ada.shape[1]
    rows = 8
    c8 = jnp.pad(c, ((0, rows - B), (0, 0)))
    out = pl.pallas_call(
        _mod_kernel,
        grid=(N // tn,),
        in_specs=[
            pl.BlockSpec((rows, D), lambda j: (0, 0)),
            pl.BlockSpec((D, tn), lambda j: (0, j)),
            pl.BlockSpec((1, tn), lambda j: (0, j)),
        ],
        out_specs=pl.BlockSpec((rows, tn), lambda j: (0, j)),
        out_shape=jax.ShapeDtypeStruct((rows, N), F32),
        compiler_params=pltpu.CompilerParams(
            dimension_semantics=("arbitrary",),
            vmem_limit_bytes=V7X_VMEM_LIMIT_BYTES),
        name="adaln_mod",
    )(c8, w_ada, b_ada.reshape(1, N))
    return out[:B]


def _inproj_kernel(x_ref, gain_ref, scale_ref, shift_ref, w_ref, o_ref, h_ref, *,
                   row_chunk):
    tm, D = x_ref.shape

    @pl.when(pl.program_id(1) == 0)
    def _():
        gain = gain_ref[...]
        one_plus_scale = 1.0 + scale_ref[0]
        shift = shift_ref[0]

        def body(ci, carry):
            r0 = pl.multiple_of(ci * row_chunk, row_chunk)
            xc = x_ref[pl.ds(r0, row_chunk), :]
            rinv = lax.rsqrt(jnp.mean(xc * xc, axis=-1, keepdims=True) + EPS)
            hc = (xc * rinv * gain) * one_plus_scale + shift
            h_ref[pl.ds(r0, row_chunk), :] = hc.astype(BF16)
            return carry

        lax.fori_loop(0, tm // row_chunk, body, 0)

    o_ref[...] = jnp.dot(h_ref[...], w_ref[...],
                         preferred_element_type=F32).astype(o_ref.dtype)


def _in_proj(x2, gain, scale, shift, w_bf, seq, tm=1024, tn=768, row_chunk=64):
    M, D = x2.shape
    N = w_bf.shape[1]
    B = scale.shape[0]
    tiles_per_seq = seq // tm
    kern = functools.partial(_inproj_kernel, row_chunk=row_chunk)
    return pl.pallas_call(
        kern,
        grid=(M // tm, N // tn),
        in_specs=[
            pl.BlockSpec((tm, D), lambda i, j: (i, 0), pipeline_mode=pl.Buffered(1)),
            pl.BlockSpec((1, D), lambda i, j: (0, 0)),
            pl.BlockSpec((1, 1, D), lambda i, j: (i // tiles_per_seq, 0, 0)),
            pl.BlockSpec((1, 1, D), lambda i, j: (i // tiles_per_seq, 0, 0)),
            pl.BlockSpec((D, tn), lambda i, j: (0, j)),
        ],
        out_specs=pl.BlockSpec((tm, tn), lambda i, j: (i, j)),
        out_shape=jax.ShapeDtypeStruct((M, N), BF16),
        scratch_shapes=[pltpu.VMEM((tm, D), BF16)],
        compiler_params=pltpu.CompilerParams(
            dimension_semantics=("arbitrary", "arbitrary"),
            vmem_limit_bytes=V7X_VMEM_LIMIT_BYTES),
        name="in_proj",
    )(x2, gain.reshape(1, D), scale.reshape(B, 1, D), shift.reshape(B, 1, D), w_bf)


def _attn_kernel(slopes_ref, q_ref, k_ref, v_ref, g_ref, o_ref,
                 qf, kf, vf, q4f, k4f, v4f,
                 q4, k4, v4, q16, k16, v16, k1, v1,
                 o1, o4, o16, l1, l4, l16):
    T = ATTN_ROWS
    QB = Q_BLOCK
    h = pl.program_id(1)
    j = pl.program_id(2)
    slope = slopes_ref[h]
    sm_scale = 1.0 / math.sqrt(HEAD_DIM)

    @pl.when(j == 0)
    def _():
        zero = jnp.zeros((QB, HEAD_DIM), BF16)
        k1[0:QB, :] = zero
        v1[0:QB, :] = zero
        for r in range(4):
            k4[r, 0:QB, :] = zero
            v4[r, 0:QB, :] = zero
        for r in range(16):
            k16[r, 0:QB, :] = zero
            v16[r, 0:QB, :] = zero

    @pl.when(j > 0)
    def _():
        k1[0:QB, :] = k1[T:T + QB, :]
        v1[0:QB, :] = v1[T:T + QB, :]
        for r in range(4):
            k4[r, 0:QB, :] = k4[r, T // 4:T // 4 + QB, :]
            v4[r, 0:QB, :] = v4[r, T // 4:T // 4 + QB, :]
        for r in range(16):
            k16[r, 0:QB, :] = k16[r, QB:2 * QB, :]
            v16[r, 0:QB, :] = v16[r, QB:2 * QB, :]

    k1[QB:QB + T, :] = k_ref[...]
    v1[QB:QB + T, :] = v_ref[...]
    qf[...] = q_ref[...].astype(F32)
    kf[...] = k_ref[...].astype(F32)
    vf[...] = v_ref[...].astype(F32)
    L4 = T // 4
    for r in range(4):
        a = qf[pl.ds(r, L4, stride=4), :]
        q4f[r] = a
        q4[r] = a.astype(BF16)
        a = kf[pl.ds(r, L4, stride=4), :]
        k4f[r] = a
        k4[r, QB:QB + L4, :] = a.astype(BF16)
        a = vf[pl.ds(r, L4, stride=4), :]
        v4f[r] = a
        v4[r, QB:QB + L4, :] = a.astype(BF16)
    for r in range(16):
        r4, a4 = r % 4, r // 4
        q16[r] = q4f[r4, pl.ds(a4, QB, stride=4), :].astype(BF16)
        k16[r, QB:2 * QB, :] = k4f[r4, pl.ds(a4, QB, stride=4), :].astype(BF16)
        v16[r, QB:2 * QB, :] = v4f[r4, pl.ds(a4, QB, stride=4), :].astype(BF16)

    qi = lax.broadcasted_iota(jnp.int32, (QB, 2 * QB), 0)
    ki = lax.broadcasted_iota(jnp.int32, (QB, 2 * QB), 1)
    rel = QB + qi - ki
    valid = (rel >= 0) & (rel <= Q_BLOCK)
    rel_f = rel.astype(F32)
    neg_inf = jnp.float32(-jnp.inf)
    first_valid = valid & (ki >= QB * (1 - jnp.minimum(j, 1)))

    def unit(q, k2, v2, mask_bias):
        s = lax.dot_general(q, k2, (((1,), (1,)), ((), ())),
                            preferred_element_type=F32)
        s = s * sm_scale + mask_bias
        m = jnp.max(s, axis=-1, keepdims=True)
        p = jnp.exp(s - m)
        l = jnp.sum(p, axis=-1, keepdims=True)
        o = jnp.dot(p.astype(BF16), v2, preferred_element_type=F32)
        o = o * (1.0 / l)
        lse = m + jnp.log(l)
        return o, jnp.broadcast_to(lse, (QB, HEAD_DIM))

    def run_pattern(d, q_get, kv_get, o_s, l_s):
        nblk = T // (d * QB)
        bias = rel_f * (-(slope * float(d)))
        mb = jnp.where(valid, bias, neg_inf)
        mb_first = jnp.where(first_valid, bias, neg_inf)

        def do(r, n, mbias):
            q = q_get(r, n)
            k2, v2 = kv_get(r, n)
            o, lse = unit(q, k2, v2, mbias)
            start = n * (QB * d) + r
            if d == 1:
                start = pl.multiple_of(start, QB)
                o_s[pl.ds(start, QB), :] = o
                l_s[pl.ds(start, QB), :] = lse
            else:
                o_s[pl.ds(start, QB, stride=d), :] = o
                l_s[pl.ds(start, QB, stride=d), :] = lse

        def first_body(r, carry):
            do(r, 0, mb_first)
            return carry

        lax.fori_loop(0, d, first_body, 0)

        if nblk > 1:
            def rest_body(u, carry):
                r = u // (nblk - 1)
                n = u % (nblk - 1) + 1
                do(r, n, mb)
                return carry

            lax.fori_loop(0, d * (nblk - 1), rest_body, 0)

    def q1_get(r, n):
        return q_ref[pl.ds(pl.multiple_of(n * QB, QB), QB), :]

    def kv1_get(r, n):
        s0 = pl.multiple_of(n * QB, QB)
        return k1[pl.ds(s0, 2 * QB), :], v1[pl.ds(s0, 2 * QB), :]

    def q4_get(r, n):
        return q4[r, pl.ds(pl.multiple_of(n * QB, QB), QB), :]

    def kv4_get(r, n):
        s0 = pl.multiple_of(n * QB, QB)
        return k4[r, pl.ds(s0, 2 * QB), :], v4[r, pl.ds(s0, 2 * QB), :]

    def q16_get(r, n):
        return q16[r]

    def kv16_get(r, n):
        return k16[r], v16[r]

    run_pattern(1, q1_get, kv1_get, o1, l1)
    run_pattern(4, q4_get, kv4_get, o4, l4)
    run_pattern(16, q16_get, kv16_get, o16, l16)

    def merge_body(ci, carry):
        r0 = pl.multiple_of(ci * QB, QB)
        rows = pl.ds(r0, QB)
        a1, a4, a16 = l1[rows, :], l4[rows, :], l16[rows, :]
        mx = jnp.maximum(jnp.maximum(a1, a4), a16)
        e1, e4, e16 = jnp.exp(a1 - mx), jnp.exp(a4 - mx), jnp.exp(a16 - mx)
        den = e1 + e4 + e16
        num = e1 * o1[rows, :] + e4 * o4[rows, :] + e16 * o16[rows, :]
        g = g_ref[rows, :].astype(F32)
        o_ref[rows, :] = (num * (1.0 / den) * _silu(g)).astype(o_ref.dtype)
        return carry

    lax.fori_loop(0, T // QB, merge_body, 0)


def _dilated_attn(proj, slopes, batch, seq, n_heads):
    M = proj.shape[0]
    T = ATTN_ROWS
    E = HEAD_DIM
    nj = seq // T
    d_attn = n_heads * E

    def col_spec(col0):
        return pl.BlockSpec((T, E), lambda b, h, j: (b * nj + j, col0 + h))

    vm = pltpu.VMEM
    scratch = [
        vm((T, E), F32), vm((T, E), F32), vm((T, E), F32),
        vm((4, T // 4, E), F32), vm((4, T // 4, E), F32), vm((4, T // 4, E), F32),
        vm((4, T // 4, E), BF16),
        vm((4, ATTN_TAIL + T // 4, E), BF16), vm((4, ATTN_TAIL + T // 4, E), BF16),
        vm((16, T // 16, E), BF16),
        vm((16, ATTN_TAIL + T // 16, E), BF16), vm((16, ATTN_TAIL + T // 16, E), BF16),
        vm((ATTN_TAIL + T, E), BF16), vm((ATTN_TAIL + T, E), BF16),
        vm((T, E), F32), vm((T, E), F32), vm((T, E), F32),
        vm((T, E), F32), vm((T, E), F32), vm((T, E), F32),
    ]
    return pl.pallas_call(
        _attn_kernel,
        grid=(batch, n_heads, nj),
        in_specs=[
            pl.BlockSpec(memory_space=pltpu.SMEM),
            col_spec(0),
            col_spec(n_heads),
            col_spec(2 * n_heads),
            col_spec(3 * n_heads),
        ],
        out_specs=pl.BlockSpec((T, E), lambda b, h, j: (b * nj + j, h)),
        out_shape=jax.ShapeDtypeStruct((M, d_attn), BF16),
        scratch_shapes=scratch,
        compiler_params=pltpu.CompilerParams(
            dimension_semantics=("arbitrary", "arbitrary", "arbitrary"),
            vmem_limit_bytes=V7X_VMEM_LIMIT_BYTES),
        name="dilated_attn",
    )(slopes, proj, proj, proj, proj)


POOL_HALO = 16


def _pool_kernel(u_ref, gp_ref, w_ref, ps_ref, o_ref, ext):
    T, C = u_ref.shape
    g = pl.program_id(1)
    t = pl.program_id(2)

    @pl.when(t == 0)
    def _():
        ext[0:POOL_HALO, :] = jnp.zeros((POOL_HALO, C), F32)

    @pl.when(t > 0)
    def _():
        ext[0:POOL_HALO, :] = ext[T:T + POOL_HALO, :]

    ext[POOL_HALO:POOL_HALO + T, :] = u_ref[...].astype(F32)
    pos = lax.broadcasted_iota(jnp.int32, (T, 1), 0) + (t * T + 1)

    for gi, win in enumerate(POOL_SIZES):
        @pl.when(g == gi)
        def _(win=win):
            acc = ext[POOL_HALO:POOL_HALO + T, :]
            for back in range(1, win):
                acc = acc + ext[POOL_HALO - back:POOL_HALO - back + T, :]
            count = jnp.minimum(pos, win).astype(F32)
            pooled = acc / count - ext[POOL_HALO:POOL_HALO + T, :]
            z = jnp.dot(pooled.astype(BF16), w_ref[0], preferred_element_type=F32)
            gate = _silu(gp_ref[...].astype(F32))
            o_ref[...] = (z * ps_ref[0] * gate).astype(o_ref.dtype)


def _pool_mixer(proj, w_pool_bf, pool_scale, batch, seq, u_col0, g_col0, tile=1024):
    M = proj.shape[0]
    G, C, _ = w_pool_bf.shape
    nt = seq // tile
    return pl.pallas_call(
        _pool_kernel,
        grid=(batch, G, nt),
        in_specs=[
            pl.BlockSpec((tile, C), lambda b, g, t: (b * nt + t, u_col0 + g)),
            pl.BlockSpec((tile, C), lambda b, g, t: (b * nt + t, g_col0 + g)),
            pl.BlockSpec((1, C, C), lambda b, g, t: (g, 0, 0)),
            pl.BlockSpec((1, 1, C), lambda b, g, t: (g, 0, 0)),
        ],
        out_specs=pl.BlockSpec((tile, C), lambda b, g, t: (b * nt + t, g)),
        out_shape=jax.ShapeDtypeStruct((M, G * C), BF16),
        scratch_shapes=[pltpu.VMEM((POOL_HALO + tile, C), F32)],
        compiler_params=pltpu.CompilerParams(
            dimension_semantics=("arbitrary", "arbitrary", "arbitrary"),
            vmem_limit_bytes=V7X_VMEM_LIMIT_BYTES),
        name="pool_mixer",
    )(proj, proj, w_pool_bf, pool_scale.reshape(G, 1, C))


def _outproj_kernel(ya_ref, yp_ref, w_ref, x_ref, gate_ref, fg_ref, o_ref, *, nk_attn):
    k = pl.program_id(1)
    nk = pl.num_programs(1)

    @pl.when(k == 0)
    def _():
        o_ref[...] = jnp.zeros_like(o_ref)

    @pl.when(k < nk_attn)
    def _():
        o_ref[...] += jnp.dot(ya_ref[...], w_ref[...], preferred_element_type=F32)

    @pl.when(k >= nk_attn)
    def _():
        o_ref[...] += jnp.dot(yp_ref[...], w_ref[...], preferred_element_type=F32)

    @pl.when(k == nk - 1)
    def _():
        xn = x_ref[...] + gate_ref[0] * o_ref[...]
        rinv = lax.rsqrt(jnp.mean(xn * xn, axis=-1, keepdims=True) + EPS)
        o_ref[...] = xn * rinv * fg_ref[...]


def _out_proj(y_attn, y_pool, w_bf, x2, gate, final_gain, seq, tm=512, tk=512):
    M, D = x2.shape
    B = gate.shape[0]
    nk_attn = y_attn.shape[1] // tk
    nk = nk_attn + y_pool.shape[1] // tk
    tiles_per_seq = seq // tm
    kern = functools.partial(_outproj_kernel, nk_attn=nk_attn)
    return pl.pallas_call(
        kern,
        grid=(M // tm, nk),
        in_specs=[
            pl.BlockSpec((tm, tk), lambda i, k: (i, jnp.minimum(k, nk_attn - 1))),
            pl.BlockSpec((tm, tk), lambda i, k: (i, jnp.maximum(k - nk_attn, 0))),
            pl.BlockSpec((tk, D), lambda i, k: (k, 0)),
            pl.BlockSpec((tm, D), lambda i, k: (i, 0)),
            pl.BlockSpec((1, 1, D), lambda i, k: (i // tiles_per_seq, 0, 0)),
            pl.BlockSpec((1, D), lambda i, k: (0, 0)),
        ],
        out_specs=pl.BlockSpec((tm, D), lambda i, k: (i, 0)),
        out_shape=jax.ShapeDtypeStruct((M, D), F32),
        compiler_params=pltpu.CompilerParams(
            dimension_semantics=("arbitrary", "arbitrary"),
            vmem_limit_bytes=V7X_VMEM_LIMIT_BYTES),
        name="out_proj",
    )(y_attn, y_pool, w_bf, x2, gate.reshape(B, 1, D), final_gain.reshape(1, D))


def kernel(x, c, norm_gain, w_ada, b_ada, w_in, w_pool, pool_scale, w_out, final_gain):
    B, S, D = x.shape
    depth = w_in.shape[0]
    assert depth == 1, "out_proj fuses the final rmsnorm, so only one layer is supported"
    d_attn = D // 2
    n_heads = d_attn // HEAD_DIM
    n_groups = len(POOL_SIZES)
    group_dim = (D - d_attn) // n_groups
    slopes = 2.0 ** (-8.0 * jnp.arange(1, n_heads + 1, dtype=F32) / n_heads)

    x2 = x.reshape(B * S, D)
    for l in range(depth):
        mod = _adaln_mod(c, w_ada[l], b_ada[l])
        shift, scale, gate = mod[:, :D], mod[:, D:2 * D], mod[:, 2 * D:]
        proj = _in_proj(x2, norm_gain[l], scale, shift, w_in[l].astype(BF16), S)
        y_attn = _dilated_attn(proj, slopes, B, S, n_heads)
        y_pool = _pool_mixer(proj, w_pool[l].astype(BF16), pool_scale[l], B, S,
                             u_col0=4 * d_attn // group_dim,
                             g_col0=4 * d_attn // group_dim + n_groups)
        x2 = _out_proj(y_attn, y_pool, w_out[l].astype(BF16), x2, gate, final_gain, S)
    return x2.reshape(B, S, D)
```

```python
import functools
import math

import jax
import jax.numpy as jnp
from jax import lax
from jax.experimental import pallas as pl
from jax.experimental.pallas import tpu as pltpu

F32 = jnp.float32
BF16 = jnp.bfloat16

HEAD_DIM = 128
POOL_SIZES = (2, 4, 8, 16)
DILATED_PATTERNS = ((128, 1), (512, 4), (2048, 16))
Q_BLOCK = 128
EPS = 1e-6

V7X_VMEM_LIMIT_BYTES = 58 * 1024 * 1024

ATTN_ROWS = 2048
ATTN_TAIL = Q_BLOCK
ATTN_UNROLL = 16
LOG2E = 1.4426950408889634
LN2 = 0.6931471805599453


def _silu(v):
    return v * (1.0 / (1.0 + jnp.exp(-v)))


def _mod_kernel(c_ref, w_ref, b_ref, o_ref):
    ca = _silu(c_ref[...]).astype(BF16)
    w = w_ref[...].astype(BF16)
    o_ref[...] = jnp.dot(ca, w, preferred_element_type=F32) + b_ref[...]


def _adaln_mod(c, w_ada, b_ada, tn=512):
    B, D = c.shape
    N = w_ada.shape[1]
    rows = 8
    c8 = jnp.pad(c, ((0, rows - B), (0, 0)))
    out = pl.pallas_call(
        _mod_kernel,
        grid=(N // tn,),
        in_specs=[
            pl.BlockSpec((rows, D), lambda j: (0, 0)),
            pl.BlockSpec((D, tn), lambda j: (0, j)),
            pl.BlockSpec((1, tn), lambda j: (0, j)),
        ],
        out_specs=pl.BlockSpec((rows, tn), lambda j: (0, j)),
        out_shape=jax.ShapeDtypeStruct((rows, N), F32),
        compiler_params=pltpu.CompilerParams(
            dimension_semantics=("arbitrary",),
            vmem_limit_bytes=V7X_VMEM_LIMIT_BYTES),
        name="adaln_mod",
    )(c8, w_ada, b_ada.reshape(1, N))
    return out[:B]


def _inproj_kernel(x_ref, gain_ref, scale_ref, shift_ref, w_ref, o_ref, h_ref, *,
                   row_chunk):
    tm, D = x_ref.shape

    @pl.when(pl.program_id(1) == 0)
    def _():
        gain = gain_ref[...]
        one_plus_scale = 1.0 + scale_ref[0]
        shift = shift_ref[0]

        def body(ci, carry):
            r0 = pl.multiple_of(ci * row_chunk, row_chunk)
            xc = x_ref[pl.ds(r0, row_chunk), :]
            rinv = lax.rsqrt(jnp.mean(xc * xc, axis=-1, keepdims=True) + EPS)
            hc = (xc * rinv * gain) * one_plus_scale + shift
            h_ref[pl.ds(r0, row_chunk), :] = hc.astype(BF16)
            return carry

        lax.fori_loop(0, tm // row_chunk, body, 0)

    o_ref[...] = jnp.dot(h_ref[...], w_ref[...],
                         preferred_element_type=F32).astype(o_ref.dtype)


def _in_proj(x2, gain, scale, shift, w_bf, seq, tm=1024, tn=768, row_chunk=64):
    M, D = x2.shape
    N = w_bf.shape[1]
    B = scale.shape[0]
    tiles_per_seq = seq // tm
    kern = functools.partial(_inproj_kernel, row_chunk=row_chunk)
    return pl.pallas_call(
        kern,
        grid=(M // tm, N // tn),
        in_specs=[
            pl.BlockSpec((tm, D), lambda i, j: (i, 0), pipeline_mode=pl.Buffered(1)),
            pl.BlockSpec((1, D), lambda i, j: (0, 0)),
            pl.BlockSpec((1, 1, D), lambda i, j: (i // tiles_per_seq, 0, 0)),
            pl.BlockSpec((1, 1, D), lambda i, j: (i // tiles_per_seq, 0, 0)),
            pl.BlockSpec((D, tn), lambda i, j: (0, j)),
        ],
        out_specs=pl.BlockSpec((tm, tn), lambda i, j: (i, j)),
        out_shape=jax.ShapeDtypeStruct((M, N), BF16),
        scratch_shapes=[pltpu.VMEM((tm, D), BF16)],
        compiler_params=pltpu.CompilerParams(
            dimension_semantics=("arbitrary", "arbitrary"),
            vmem_limit_bytes=V7X_VMEM_LIMIT_BYTES),
        name="in_proj",
    )(x2, gain.reshape(1, D), scale.reshape(B, 1, D), shift.reshape(B, 1, D), w_bf)


def _attn_kernel(slopes_ref, q_ref, k_ref, v_ref, g_ref, o_ref,
                 qf, kf, vf, q4f, k4f, v4f,
                 q4, k4, v4, q16, k16, v16, k1, v1,
                 o1, o4, o16, l1, l4, l16, s_s, m_s, mb_s):
    T = ATTN_ROWS
    QB = Q_BLOCK
    h = pl.program_id(1)
    j = pl.program_id(2)
    slope = slopes_ref[h]
    sm_scale = 1.0 / math.sqrt(HEAD_DIM)

    @pl.when(j == 0)
    def _():
        zero = jnp.zeros((QB, HEAD_DIM), BF16)
        k1[0:QB, :] = zero
        v1[0:QB, :] = zero
        for r in range(4):
            k4[r, 0:QB, :] = zero
            v4[r, 0:QB, :] = zero
        for r in range(16):
            k16[r, 0:QB, :] = zero
            v16[r, 0:QB, :] = zero

    @pl.when(j > 0)
    def _():
        k1[0:QB, :] = k1[T:T + QB, :]
        v1[0:QB, :] = v1[T:T + QB, :]
        for r in range(4):
            k4[r, 0:QB, :] = k4[r, T // 4:T // 4 + QB, :]
            v4[r, 0:QB, :] = v4[r, T // 4:T // 4 + QB, :]
        for r in range(16):
            k16[r, 0:QB, :] = k16[r, QB:2 * QB, :]
            v16[r, 0:QB, :] = v16[r, QB:2 * QB, :]

    k1[QB:QB + T, :] = k_ref[...]
    v1[QB:QB + T, :] = v_ref[...]
    qf[...] = q_ref[...].astype(F32)
    kf[...] = k_ref[...].astype(F32)
    vf[...] = v_ref[...].astype(F32)
    L4 = T // 4
    for r in range(4):
        a = qf[pl.ds(r, L4, stride=4), :]
        q4f[r] = a
        q4[r] = a.astype(BF16)
        a = kf[pl.ds(r, L4, stride=4), :]
        k4f[r] = a
        k4[r, QB:QB + L4, :] = a.astype(BF16)
        a = vf[pl.ds(r, L4, stride=4), :]
        v4f[r] = a
        v4[r, QB:QB + L4, :] = a.astype(BF16)
    for r in range(16):
        r4, a4 = r % 4, r // 4
        q16[r] = q4f[r4, pl.ds(a4, QB, stride=4), :].astype(BF16)
        k16[r, QB:2 * QB, :] = k4f[r4, pl.ds(a4, QB, stride=4), :].astype(BF16)
        v16[r, QB:2 * QB, :] = v4f[r4, pl.ds(a4, QB, stride=4), :].astype(BF16)

    qi = lax.broadcasted_iota(jnp.int32, (QB, 2 * QB), 0)
    ki = lax.broadcasted_iota(jnp.int32, (QB, 2 * QB), 1)
    rel = QB + qi - ki
    valid = (rel >= 0) & (rel <= Q_BLOCK)
    rel_f = rel.astype(F32)
    neg_inf = jnp.float32(-jnp.inf)
    first_step = 1 - jnp.minimum(j, 1)
    first_valid = valid & (ki >= QB * first_step)

    n_units = T // QB
    ones_kv = jnp.ones((2 * QB, HEAD_DIM), BF16)

    def run_pattern(d, q_get, k_get, v_get, o_s, l_s):
        nblk = T // (d * QB)

        def r_n(idx):
            return idx // nblk, idx % nblk

        bias = rel_f * (-(slope * float(d) * LOG2E))
        mb_s[0] = jnp.where(valid, bias, neg_inf)
        mb_s[1] = jnp.where(first_valid, bias, neg_inf)

        def score_body(it, carry):
            for u in range(ATTN_UNROLL):
                idx = it * ATTN_UNROLL + u
                r, n = r_n(idx)
                s = lax.dot_general(q_get(r, n), k_get(r, n),
                                    (((1,), (1,)), ((), ())),
                                    preferred_element_type=F32)
                variant = 1 - jnp.minimum(n, 1)
                t = s * (sm_scale * LOG2E) + mb_s[variant]
                s_s[idx] = t
                m = jnp.max(t, axis=-1, keepdims=True)
                m_s[idx] = jnp.broadcast_to(m, (QB, HEAD_DIM))
            return carry

        lax.fori_loop(0, n_units // ATTN_UNROLL, score_body, 0)

        def value_body(it, carry):
            for u in range(ATTN_UNROLL):
                idx = it * ATTN_UNROLL + u
                r, n = r_n(idx)
                m = m_s[idx]
                p = jnp.exp2(s_s[idx] - jnp.concatenate([m, m], axis=-1)).astype(BF16)
                ol = jnp.dot(p, jnp.concatenate([v_get(r, n), ones_kv], axis=-1),
                             preferred_element_type=F32)
                o, l = ol[:, :HEAD_DIM], ol[:, HEAD_DIM:]
                o = o * (1.0 / l)
                lse = m_s[idx] * LN2 + jnp.log(l)
                start = n * (QB * d) + r
                if d == 1:
                    start = pl.multiple_of(start, QB)
                    o_s[pl.ds(start, QB), :] = o
                    l_s[pl.ds(start, QB), :] = lse
                else:
                    o_s[pl.ds(start, QB, stride=d), :] = o
                    l_s[pl.ds(start, QB, stride=d), :] = lse
            return carry

        lax.fori_loop(0, n_units // ATTN_UNROLL, value_body, 0)

    def blk(n):
        return pl.multiple_of(n * QB, QB)

    run_pattern(1,
                lambda r, n: q_ref[pl.ds(blk(n), QB), :],
                lambda r, n: k1[pl.ds(blk(n), 2 * QB), :],
                lambda r, n: v1[pl.ds(blk(n), 2 * QB), :], o1, l1)
    run_pattern(4,
                lambda r, n: q4[r, pl.ds(blk(n), QB), :],
                lambda r, n: k4[r, pl.ds(blk(n), 2 * QB), :],
                lambda r, n: v4[r, pl.ds(blk(n), 2 * QB), :], o4, l4)
    run_pattern(16,
                lambda r, n: q16[r],
                lambda r, n: k16[r],
                lambda r, n: v16[r], o16, l16)

    def merge_body(ci, carry):
        r0 = pl.multiple_of(ci * QB, QB)
        rows = pl.ds(r0, QB)
        a1, a4, a16 = l1[rows, :], l4[rows, :], l16[rows, :]
        mx = jnp.maximum(jnp.maximum(a1, a4), a16)
        e1, e4, e16 = jnp.exp(a1 - mx), jnp.exp(a4 - mx), jnp.exp(a16 - mx)
        den = e1 + e4 + e16
        num = e1 * o1[rows, :] + e4 * o4[rows, :] + e16 * o16[rows, :]
        g = g_ref[rows, :].astype(F32)
        o_ref[rows, :] = (num * (1.0 / den) * _silu(g)).astype(o_ref.dtype)
        return carry

    lax.fori_loop(0, T // QB, merge_body, 0)


def _dilated_attn(proj, slopes, batch, seq, n_heads):
    M = proj.shape[0]
    T = ATTN_ROWS
    E = HEAD_DIM
    nj = seq // T
    d_attn = n_heads * E

    def col_spec(col0):
        return pl.BlockSpec((T, E), lambda b, h, j: (b * nj + j, col0 + h))

    vm = pltpu.VMEM
    scratch = [
        vm((T, E), F32), vm((T, E), F32), vm((T, E), F32),
        vm((4, T // 4, E), F32), vm((4, T // 4, E), F32), vm((4, T // 4, E), F32),
        vm((4, T // 4, E), BF16),
        vm((4, ATTN_TAIL + T // 4, E), BF16), vm((4, ATTN_TAIL + T // 4, E), BF16),
        vm((16, T // 16, E), BF16),
        vm((16, ATTN_TAIL + T // 16, E), BF16), vm((16, ATTN_TAIL + T // 16, E), BF16),
        vm((ATTN_TAIL + T, E), BF16), vm((ATTN_TAIL + T, E), BF16),
        vm((T, E), F32), vm((T, E), F32), vm((T, E), F32),
        vm((T, E), F32), vm((T, E), F32), vm((T, E), F32),
        vm((T // Q_BLOCK, Q_BLOCK, 2 * Q_BLOCK), F32),
        vm((T // Q_BLOCK, Q_BLOCK, E), F32),
        vm((2, Q_BLOCK, 2 * Q_BLOCK), F32),
    ]
    return pl.pallas_call(
        _attn_kernel,
        grid=(batch, n_heads, nj),
        in_specs=[
            pl.BlockSpec(memory_space=pltpu.SMEM),
            col_spec(0),
            col_spec(n_heads),
            col_spec(2 * n_heads),
            col_spec(3 * n_heads),
        ],
        out_specs=pl.BlockSpec((T, E), lambda b, h, j: (b * nj + j, h)),
        out_shape=jax.ShapeDtypeStruct((M, d_attn), BF16),
        scratch_shapes=scratch,
        compiler_params=pltpu.CompilerParams(
            dimension_semantics=("arbitrary", "arbitrary", "arbitrary"),
            vmem_limit_bytes=V7X_VMEM_LIMIT_BYTES),
        name="dilated_attn",
    )(slopes, proj, proj, proj, proj)


POOL_HALO = 16


def _pool_kernel(u_ref, gp_ref, w_ref, ps_ref, o_ref, ext):
    T, C = u_ref.shape
    g = pl.program_id(1)
    t = pl.program_id(2)

    @pl.when(t == 0)
    def _():
        ext[0:POOL_HALO, :] = jnp.zeros((POOL_HALO, C), F32)

    @pl.when(t > 0)
    def _():
        ext[0:POOL_HALO, :] = ext[T:T + POOL_HALO, :]

    ext[POOL_HALO:POOL_HALO + T, :] = u_ref[...].astype(F32)
    pos = lax.broadcasted_iota(jnp.int32, (T, 1), 0) + (t * T + 1)

    for gi, win in enumerate(POOL_SIZES):
        @pl.when(g == gi)
        def _(win=win):
            acc = ext[POOL_HALO:POOL_HALO + T, :]
            for back in range(1, win):
                acc = acc + ext[POOL_HALO - back:POOL_HALO - back + T, :]
            count = jnp.minimum(pos, win).astype(F32)
            pooled = acc / count - ext[POOL_HALO:POOL_HALO + T, :]
            z = jnp.dot(pooled.astype(BF16), w_ref[0], preferred_element_type=F32)
            gate = _silu(gp_ref[...].astype(F32))
            o_ref[...] = (z * ps_ref[0] * gate).astype(o_ref.dtype)


def _pool_mixer(proj, w_pool_bf, pool_scale, batch, seq, u_col0, g_col0, tile=1024):
    M = proj.shape[0]
    G, C, _ = w_pool_bf.shape
    nt = seq // tile
    return pl.pallas_call(
        _pool_kernel,
        grid=(batch, G, nt),
        in_specs=[
            pl.BlockSpec((tile, C), lambda b, g, t: (b * nt + t, u_col0 + g)),
            pl.BlockSpec((tile, C), lambda b, g, t: (b * nt + t, g_col0 + g)),
            pl.BlockSpec((1, C, C), lambda b, g, t: (g, 0, 0)),
            pl.BlockSpec((1, 1, C), lambda b, g, t: (g, 0, 0)),
        ],
        out_specs=pl.BlockSpec((tile, C), lambda b, g, t: (b * nt + t, g)),
        out_shape=jax.ShapeDtypeStruct((M, G * C), BF16),
        scratch_shapes=[pltpu.VMEM((POOL_HALO + tile, C), F32)],
        compiler_params=pltpu.CompilerParams(
            dimension_semantics=("arbitrary", "arbitrary", "arbitrary"),
            vmem_limit_bytes=V7X_VMEM_LIMIT_BYTES),
        name="pool_mixer",
    )(proj, proj, w_pool_bf, pool_scale.reshape(G, 1, C))


def _outproj_kernel(ya_ref, yp_ref, w_ref, x_ref, gate_ref, fg_ref, o_ref, *, nk_attn):
    k = pl.program_id(1)
    nk = pl.num_programs(1)

    @pl.when(k == 0)
    def _():
        o_ref[...] = jnp.dot(ya_ref[...], w_ref[...], preferred_element_type=F32)

    @pl.when((k > 0) & (k < nk_attn))
    def _():
        o_ref[...] += jnp.dot(ya_ref[...], w_ref[...], preferred_element_type=F32)

    @pl.when(k >= nk_attn)
    def _():
        o_ref[...] += jnp.dot(yp_ref[...], w_ref[...], preferred_element_type=F32)

    @pl.when(k == nk - 1)
    def _():
        xn = x_ref[...] + gate_ref[0] * o_ref[...]
        rinv = lax.rsqrt(jnp.mean(xn * xn, axis=-1, keepdims=True) + EPS)
        o_ref[...] = xn * rinv * fg_ref[...]


def _out_proj(y_attn, y_pool, w_bf, x2, gate, final_gain, seq, tm=512, tk=512):
    M, D = x2.shape
    B = gate.shape[0]
    nk_attn = y_attn.shape[1] // tk
    nk = nk_attn + y_pool.shape[1] // tk
    tiles_per_seq = seq // tm
    kern = functools.partial(_outproj_kernel, nk_attn=nk_attn)
    return pl.pallas_call(
        kern,
        grid=(M // tm, nk),
        in_specs=[
            pl.BlockSpec((tm, tk), lambda i, k: (i, jnp.minimum(k, nk_attn - 1))),
            pl.BlockSpec((tm, tk), lambda i, k: (i, jnp.maximum(k - nk_attn, 0))),
            pl.BlockSpec((tk, D), lambda i, k: (k, 0)),
            pl.BlockSpec((tm, D), lambda i, k: (i, 0)),
            pl.BlockSpec((1, 1, D), lambda i, k: (i // tiles_per_seq, 0, 0)),
            pl.BlockSpec((1, D), lambda i, k: (0, 0)),
        ],
        out_specs=pl.BlockSpec((tm, D), lambda i, k: (i, 0)),
        out_shape=jax.ShapeDtypeStruct((M, D), F32),
        compiler_params=pltpu.CompilerParams(
            dimension_semantics=("arbitrary", "arbitrary"),
            vmem_limit_bytes=V7X_VMEM_LIMIT_BYTES),
        name="out_proj",
    )(y_attn, y_pool, w_bf, x2, gate.reshape(B, 1, D), final_gain.reshape(1, D))


def kernel(x, c, norm_gain, w_ada, b_ada, w_in, w_pool, pool_scale, w_out, final_gain):
    B, S, D = x.shape
    depth = w_in.shape[0]
    assert depth == 1, "out_proj fuses the final rmsnorm, so only one layer is supported"
    d_attn = D // 2
    n_heads = d_attn // HEAD_DIM
    n_groups = len(POOL_SIZES)
    group_dim = (D - d_attn) // n_groups
    slopes = 2.0 ** (-8.0 * jnp.arange(1, n_heads + 1, dtype=F32) / n_heads)

    x2 = x.reshape(B * S, D)
    for l in range(depth):
        mod = _adaln_mod(c, w_ada[l], b_ada[l])
        shift, scale, gate = mod[:, :D], mod[:, D:2 * D], mod[:, 2 * D:]
        proj = _in_proj(x2, norm_gain[l], scale, shift, w_in[l].astype(BF16), S)
        y_attn = _dilated_attn(proj, slopes, B, S, n_heads)
        y_pool = _pool_mixer(proj, w_pool[l].astype(BF16), pool_scale[l], B, S,
                             u_col0=4 * d_attn // group_dim,
                             g_col0=4 * d_attn // group_dim + n_groups)
        x2 = _out_proj(y_attn, y_pool, w_out[l].astype(BF16), x2, gate, final_gain, S)
    return x2.reshape(B, S, D)
```

```python
import functools
import math

import jax
import jax.numpy as jnp
from jax import lax
from jax.experimental import pallas as pl
from jax.experimental.pallas import tpu as pltpu

F32 = jnp.float32
BF16 = jnp.bfloat16

HEAD_DIM = 128
POOL_SIZES = (2, 4, 8, 16)
DILATED_PATTERNS = ((128, 1), (512, 4), (2048, 16))
Q_BLOCK = 128
EPS = 1e-6

V7X_VMEM_LIMIT_BYTES = 58 * 1024 * 1024

ATTN_ROWS = 2048
ATTN_TAIL = Q_BLOCK
ATTN_UNROLL = 16
LOG2E = 1.4426950408889634
LN2 = 0.6931471805599453


def _silu(v):
    return v * (1.0 / (1.0 + jnp.exp(-v)))


def _mod_kernel(c_ref, w_ref, b_ref, o_ref):
    ca = _silu(c_ref[...]).astype(BF16)
    w = w_ref[...].astype(BF16)
    o_ref[...] = jnp.dot(ca, w, preferred_element_type=F32) + b_ref[...]


def _adaln_mod(c, w_ada, b_ada, tn=512):
    B, D = c.shape
    N = w_ada.shape[1]
    rows = 8
    c8 = jnp.pad(c, ((0, rows - B), (0, 0)))
    out = pl.pallas_call(
        _mod_kernel,
        grid=(N // tn,),
        in_specs=[
            pl.BlockSpec((rows, D), lambda j: (0, 0)),
            pl.BlockSpec((D, tn), lambda j: (0, j)),
            pl.BlockSpec((1, tn), lambda j: (0, j)),
        ],
        out_specs=pl.BlockSpec((rows, tn), lambda j: (0, j)),
        out_shape=jax.ShapeDtypeStruct((rows, N), F32),
        compiler_params=pltpu.CompilerParams(
            dimension_semantics=("arbitrary",),
            vmem_limit_bytes=V7X_VMEM_LIMIT_BYTES),
        name="adaln_mod",
    )(c8, w_ada, b_ada.reshape(1, N))
    return out[:B]


def _inproj_kernel(x_ref, gain_ref, scale_ref, shift_ref, w_ref, o_ref, h_ref):
    i = pl.program_id(0)
    j = pl.program_id(1)
    chunk = x_ref.shape[0]
    slot = i % 2

    def norm_chunk():
        x = x_ref[...]
        rinv = lax.rsqrt(jnp.mean(x * x, axis=-1, keepdims=True) + EPS)
        h = (x * rinv * gain_ref[...]) * (1.0 + scale_ref[0]) + shift_ref[0]
        h_ref[slot, pl.ds(pl.multiple_of(j * chunk, chunk), chunk), :] = h.astype(BF16)

    @pl.when(i == 0)
    def _():
        norm_chunk()

    @pl.when(i > 0)
    def _():
        norm_chunk()
        o_ref[...] = jnp.dot(h_ref[1 - slot], w_ref[...],
                             preferred_element_type=F32).astype(o_ref.dtype)


def _in_proj(x2, gain, scale, shift, w_bf, seq, tm=1024, tn=768):
    M, D = x2.shape
    N = w_bf.shape[1]
    B = scale.shape[0]
    n_i, n_j = M // tm, N // tn
    chunk = tm // n_j
    assert chunk * n_j == tm and chunk % 16 == 0
    tiles_per_seq = seq // tm

    def x_map(i, j):
        return (jnp.where(i < n_i, i * n_j + j, n_i * n_j - 1), 0)

    def mod_map(i, j):
        return (jnp.minimum(i, n_i - 1) // tiles_per_seq, 0, 0)

    def col_map(i, j):
        return jnp.where(i > 0, j, 0)

    return pl.pallas_call(
        _inproj_kernel,
        grid=(n_i + 1, n_j),
        in_specs=[
            pl.BlockSpec((chunk, D), x_map),
            pl.BlockSpec((1, D), lambda i, j: (0, 0)),
            pl.BlockSpec((1, 1, D), mod_map),
            pl.BlockSpec((1, 1, D), mod_map),
            pl.BlockSpec((D, tn), lambda i, j: (0, col_map(i, j))),
        ],
        out_specs=pl.BlockSpec((tm, tn),
                               lambda i, j: (jnp.maximum(i - 1, 0), col_map(i, j))),
        out_shape=jax.ShapeDtypeStruct((M, N), BF16),
        scratch_shapes=[pltpu.VMEM((2, tm, D), BF16)],
        compiler_params=pltpu.CompilerParams(
            dimension_semantics=("arbitrary", "arbitrary"),
            vmem_limit_bytes=V7X_VMEM_LIMIT_BYTES),
        name="in_proj",
    )(x2, gain.reshape(1, D), scale.reshape(B, 1, D), shift.reshape(B, 1, D), w_bf)


def _attn_kernel(slopes_ref, q_ref, k_ref, v_ref, g_ref, o_ref,
                 qf, kf, vf, q4f, k4f, v4f,
                 q4, k4, v4, q16, k16, v16, k1, v1,
                 o1, o4, o16, l1, l4, l16, s_s, m_s, mb_s):
    T = ATTN_ROWS
    QB = Q_BLOCK
    h = pl.program_id(1)
    j = pl.program_id(2)
    slope = slopes_ref[h]
    sm_scale = 1.0 / math.sqrt(HEAD_DIM)

    @pl.when(j == 0)
    def _():
        zero = jnp.zeros((QB, HEAD_DIM), BF16)
        k1[0:QB, :] = zero
        v1[0:QB, :] = zero
        for r in range(4):
            k4[r, 0:QB, :] = zero
            v4[r, 0:QB, :] = zero
        for r in range(16):
            k16[r, 0:QB, :] = zero
            v16[r, 0:QB, :] = zero

    @pl.when(j > 0)
    def _():
        k1[0:QB, :] = k1[T:T + QB, :]
        v1[0:QB, :] = v1[T:T + QB, :]
        for r in range(4):
            k4[r, 0:QB, :] = k4[r, T // 4:T // 4 + QB, :]
            v4[r, 0:QB, :] = v4[r, T // 4:T // 4 + QB, :]
        for r in range(16):
            k16[r, 0:QB, :] = k16[r, QB:2 * QB, :]
            v16[r, 0:QB, :] = v16[r, QB:2 * QB, :]

    k1[QB:QB + T, :] = k_ref[...]
    v1[QB:QB + T, :] = v_ref[...]
    qf[...] = q_ref[...].astype(F32)
    kf[...] = k_ref[...].astype(F32)
    vf[...] = v_ref[...].astype(F32)
    L4 = T // 4
    for r in range(4):
        a = qf[pl.ds(r, L4, stride=4), :]
        q4f[r] = a
        q4[r] = a.astype(BF16)
        a = kf[pl.ds(r, L4, stride=4), :]
        k4f[r] = a
        k4[r, QB:QB + L4, :] = a.astype(BF16)
        a = vf[pl.ds(r, L4, stride=4), :]
        v4f[r] = a
        v4[r, QB:QB + L4, :] = a.astype(BF16)
    for r in range(16):
        r4, a4 = r % 4, r // 4
        q16[r] = q4f[r4, pl.ds(a4, QB, stride=4), :].astype(BF16)
        k16[r, QB:2 * QB, :] = k4f[r4, pl.ds(a4, QB, stride=4), :].astype(BF16)
        v16[r, QB:2 * QB, :] = v4f[r4, pl.ds(a4, QB, stride=4), :].astype(BF16)

    qi = lax.broadcasted_iota(jnp.int32, (QB, 2 * QB), 0)
    ki = lax.broadcasted_iota(jnp.int32, (QB, 2 * QB), 1)
    rel = QB + qi - ki
    valid = (rel >= 0) & (rel <= Q_BLOCK)
    rel_f = rel.astype(F32)
    neg_inf = jnp.float32(-jnp.inf)
    first_step = 1 - jnp.minimum(j, 1)
    first_valid = valid & (ki >= QB * first_step)

    n_units = T // QB
    ones_kv = jnp.ones((2 * QB, HEAD_DIM), BF16)

    def run_pattern(d, q_get, k_get, v_get, o_s, l_s):
        nblk = T // (d * QB)

        def r_n(idx):
            return idx // nblk, idx % nblk

        bias = rel_f * (-(slope * float(d) * LOG2E))
        mb_s[0] = jnp.where(valid, bias, neg_inf)
        mb_s[1] = jnp.where(first_valid, bias, neg_inf)

        def score_body(it, carry):
            for u in range(ATTN_UNROLL):
                idx = it * ATTN_UNROLL + u
                r, n = r_n(idx)
                s = lax.dot_general(q_get(r, n), k_get(r, n),
                                    (((1,), (1,)), ((), ())),
                                    preferred_element_type=F32)
                variant = 1 - jnp.minimum(n, 1)
                t = s * (sm_scale * LOG2E) + mb_s[variant]
                s_s[idx] = t
                m = jnp.max(t, axis=-1, keepdims=True)
                m_s[idx] = jnp.broadcast_to(m, (QB, HEAD_DIM))
            return carry

        lax.fori_loop(0, n_units // ATTN_UNROLL, score_body, 0)

        def value_body(it, carry):
            for u in range(ATTN_UNROLL):
                idx = it * ATTN_UNROLL + u
                r, n = r_n(idx)
                m = m_s[idx]
                p = jnp.exp2(s_s[idx] - jnp.concatenate([m, m], axis=-1)).astype(BF16)
                ol = jnp.dot(p, jnp.concatenate([v_get(r, n), ones_kv], axis=-1),
                             preferred_element_type=F32)
                o, l = ol[:, :HEAD_DIM], ol[:, HEAD_DIM:]
                o = o * (1.0 / l)
                lse = m_s[idx] * LN2 + jnp.log(l)
                start = n * (QB * d) + r
                if d == 1:
                    start = pl.multiple_of(start, QB)
                    o_s[pl.ds(start, QB), :] = o
                    l_s[pl.ds(start, QB), :] = lse
                else:
                    o_s[pl.ds(start, QB, stride=d), :] = o
                    l_s[pl.ds(start, QB, stride=d), :] = lse
            return carry

        lax.fori_loop(0, n_units // ATTN_UNROLL, value_body, 0)

    def blk(n):
        return pl.multiple_of(n * QB, QB)

    run_pattern(1,
                lambda r, n: q_ref[pl.ds(blk(n), QB), :],
                lambda r, n: k1[pl.ds(blk(n), 2 * QB), :],
                lambda r, n: v1[pl.ds(blk(n), 2 * QB), :], o1, l1)
    run_pattern(4,
                lambda r, n: q4[r, pl.ds(blk(n), QB), :],
                lambda r, n: k4[r, pl.ds(blk(n), 2 * QB), :],
                lambda r, n: v4[r, pl.ds(blk(n), 2 * QB), :], o4, l4)
    run_pattern(16,
                lambda r, n: q16[r],
                lambda r, n: k16[r],
                lambda r, n: v16[r], o16, l16)

    def merge_body(ci, carry):
        r0 = pl.multiple_of(ci * QB, QB)
        rows = pl.ds(r0, QB)
        a1, a4, a16 = l1[rows, :], l4[rows, :], l16[rows, :]
        mx = jnp.maximum(jnp.maximum(a1, a4), a16)
        e1, e4, e16 = jnp.exp(a1 - mx), jnp.exp(a4 - mx), jnp.exp(a16 - mx)
        den = e1 + e4 + e16
        num = e1 * o1[rows, :] + e4 * o4[rows, :] + e16 * o16[rows, :]
        g = g_ref[rows, :].astype(F32)
        o_ref[rows, :] = (num * (1.0 / den) * _silu(g)).astype(o_ref.dtype)
        return carry

    lax.fori_loop(0, T // QB, merge_body, 0)


def _dilated_attn(proj, slopes, batch, seq, n_heads):
    M = proj.shape[0]
    T = ATTN_ROWS
    E = HEAD_DIM
    nj = seq // T
    d_attn = n_heads * E

    def col_spec(col0):
        return pl.BlockSpec((T, E), lambda b, h, j: (b * nj + j, col0 + h))

    vm = pltpu.VMEM
    scratch = [
        vm((T, E), F32), vm((T, E), F32), vm((T, E), F32),
        vm((4, T // 4, E), F32), vm((4, T // 4, E), F32), vm((4, T // 4, E), F32),
        vm((4, T // 4, E), BF16),
        vm((4, ATTN_TAIL + T // 4, E), BF16), vm((4, ATTN_TAIL + T // 4, E), BF16),
        vm((16, T // 16, E), BF16),
        vm((16, ATTN_TAIL + T // 16, E), BF16), vm((16, ATTN_TAIL + T // 16, E), BF16),
        vm((ATTN_TAIL + T, E), BF16), vm((ATTN_TAIL + T, E), BF16),
        vm((T, E), F32), vm((T, E), F32), vm((T, E), F32),
        vm((T, E), F32), vm((T, E), F32), vm((T, E), F32),
        vm((T // Q_BLOCK, Q_BLOCK, 2 * Q_BLOCK), F32),
        vm((T // Q_BLOCK, Q_BLOCK, E), F32),
        vm((2, Q_BLOCK, 2 * Q_BLOCK), F32),
    ]
    return pl.pallas_call(
        _attn_kernel,
        grid=(batch, n_heads, nj),
        in_specs=[
            pl.BlockSpec(memory_space=pltpu.SMEM),
            col_spec(0),
            col_spec(n_heads),
            col_spec(2 * n_heads),
            col_spec(3 * n_heads),
        ],
        out_specs=pl.BlockSpec((T, E), lambda b, h, j: (b * nj + j, h)),
        out_shape=jax.ShapeDtypeStruct((M, d_attn), BF16),
        scratch_shapes=scratch,
        compiler_params=pltpu.CompilerParams(
            dimension_semantics=("arbitrary", "arbitrary", "arbitrary"),
            vmem_limit_bytes=V7X_VMEM_LIMIT_BYTES),
        name="dilated_attn",
    )(slopes, proj, proj, proj, proj)


POOL_HALO = 16


def _pool_kernel(u_ref, gp_ref, w_ref, ps_ref, o_ref, ext):
    T, C = u_ref.shape
    g = pl.program_id(1)
    t = pl.program_id(2)

    @pl.when(t == 0)
    def _():
        ext[0:POOL_HALO, :] = jnp.zeros((POOL_HALO, C), F32)

    @pl.when(t > 0)
    def _():
        ext[0:POOL_HALO, :] = ext[T:T + POOL_HALO, :]

    ext[POOL_HALO:POOL_HALO + T, :] = u_ref[...].astype(F32)
    pos = lax.broadcasted_iota(jnp.int32, (T, 1), 0) + (t * T + 1)

    for gi, win in enumerate(POOL_SIZES):
        @pl.when(g == gi)
        def _(win=win):
            acc = ext[POOL_HALO:POOL_HALO + T, :]
            for back in range(1, win):
                acc = acc + ext[POOL_HALO - back:POOL_HALO - back + T, :]
            count = jnp.minimum(pos, win).astype(F32)
            pooled = acc / count - ext[POOL_HALO:POOL_HALO + T, :]
            z = jnp.dot(pooled.astype(BF16), w_ref[0], preferred_element_type=F32)
            gate = _silu(gp_ref[...].astype(F32))
            o_ref[...] = (z * ps_ref[0] * gate).astype(o_ref.dtype)


def _pool_mixer(proj, w_pool_bf, pool_scale, batch, seq, u_col0, g_col0, tile=1024):
    M = proj.shape[0]
    G, C, _ = w_pool_bf.shape
    nt = seq // tile
    return pl.pallas_call(
        _pool_kernel,
        grid=(batch, G, nt),
        in_specs=[
            pl.BlockSpec((tile, C), lambda b, g, t: (b * nt + t, u_col0 + g)),
            pl.BlockSpec((tile, C), lambda b, g, t: (b * nt + t, g_col0 + g)),
            pl.BlockSpec((1, C, C), lambda b, g, t: (g, 0, 0)),
            pl.BlockSpec((1, 1, C), lambda b, g, t: (g, 0, 0)),
        ],
        out_specs=pl.BlockSpec((tile, C), lambda b, g, t: (b * nt + t, g)),
        out_shape=jax.ShapeDtypeStruct((M, G * C), BF16),
        scratch_shapes=[pltpu.VMEM((POOL_HALO + tile, C), F32)],
        compiler_params=pltpu.CompilerParams(
            dimension_semantics=("arbitrary", "arbitrary", "arbitrary"),
            vmem_limit_bytes=V7X_VMEM_LIMIT_BYTES),
        name="pool_mixer",
    )(proj, proj, w_pool_bf, pool_scale.reshape(G, 1, C))


def _outproj_kernel(ya_ref, yp_ref, w_ref, x_ref, gate_ref, fg_ref, o_ref, *, col_chunk):
    tm, D = x_ref.shape
    y = jnp.concatenate([ya_ref[...], yp_ref[...]], axis=-1)
    sumsq = jnp.zeros((tm, 1), F32)
    for c in range(D // col_chunk):
        cols = slice(c * col_chunk, (c + 1) * col_chunk)
        yw = jnp.dot(y, w_ref[:, cols], preferred_element_type=F32)
        xn = x_ref[:, cols] + gate_ref[0, :, cols] * yw
        o_ref[:, cols] = xn
        sumsq = sumsq + jnp.sum(xn * xn, axis=-1, keepdims=True)
    rinv = lax.rsqrt(sumsq * (1.0 / D) + EPS)
    o_ref[...] = o_ref[...] * rinv * fg_ref[...]


def _out_proj(y_attn, y_pool, w_bf, x2, gate, final_gain, seq, tm=256, col_chunk=512):
    M, D = x2.shape
    B = gate.shape[0]
    d_attn, d_pool = y_attn.shape[1], y_pool.shape[1]
    tiles_per_seq = seq // tm
    kern = functools.partial(_outproj_kernel, col_chunk=col_chunk)
    return pl.pallas_call(
        kern,
        grid=(M // tm,),
        in_specs=[
            pl.BlockSpec((tm, d_attn), lambda i: (i, 0)),
            pl.BlockSpec((tm, d_pool), lambda i: (i, 0)),
            pl.BlockSpec((d_attn + d_pool, D), lambda i: (0, 0),
                         pipeline_mode=pl.Buffered(1)),
            pl.BlockSpec((tm, D), lambda i: (i, 0)),
            pl.BlockSpec((1, 1, D), lambda i: (i // tiles_per_seq, 0, 0)),
            pl.BlockSpec((1, D), lambda i: (0, 0)),
        ],
        out_specs=pl.BlockSpec((tm, D), lambda i: (i, 0)),
        out_shape=jax.ShapeDtypeStruct((M, D), F32),
        compiler_params=pltpu.CompilerParams(
            dimension_semantics=("arbitrary",),
            vmem_limit_bytes=V7X_VMEM_LIMIT_BYTES),
        name="out_proj",
    )(y_attn, y_pool, w_bf, x2, gate.reshape(B, 1, D), final_gain.reshape(1, D))


def kernel(x, c, norm_gain, w_ada, b_ada, w_in, w_pool, pool_scale, w_out, final_gain):
    B, S, D = x.shape
    depth = w_in.shape[0]
    assert depth == 1, "out_proj fuses the final rmsnorm, so only one layer is supported"
    d_attn = D // 2
    n_heads = d_attn // HEAD_DIM
    n_groups = len(POOL_SIZES)
    group_dim = (D - d_attn) // n_groups
    slopes = 2.0 ** (-8.0 * jnp.arange(1, n_heads + 1, dtype=F32) / n_heads)

    x2 = x.reshape(B * S, D)
    for l in range(depth):
        mod = _adaln_mod(c, w_ada[l], b_ada[l])
        shift, scale, gate = mod[:, :D], mod[:, D:2 * D], mod[:, 2 * D:]
        proj = _in_proj(x2, norm_gain[l], scale, shift, w_in[l].astype(BF16), S)
        y_attn = _dilated_attn(proj, slopes, B, S, n_heads)
        y_pool = _pool_mixer(proj, w_pool[l].astype(BF16), pool_scale[l], B, S,
                             u_col0=4 * d_attn // group_dim,
                             g_col0=4 * d_attn // group_dim + n_groups)
        x2 = _out_proj(y_attn, y_pool, w_out[l].astype(BF16), x2, gate, final_gain, S)
    return x2.reshape(B, S, D)
```
